```python
import math
import jax, jax.numpy as jnp
from jax import lax
import numpy as np

D_MODEL = 2048
BATCH = 4
SEQ = 2048
DEPTH = 1
DEC_BATCH = 128
DEC_SEQ = 8
PAST_LEN = 16384
PAGE_SIZE = 128

N_MEM = 256
D_CONV = 1024
CONV_WIDTH = 31
CONV_TAIL = CONV_WIDTH - 1
HGRN_HEADS = 8
HGRN_DK = 128
HGRN_DV = 128
D_HGRN = HGRN_HEADS * HGRN_DV
D_HGRN_F = HGRN_HEADS * HGRN_DK
ATTN_HEADS = 4
ATTN_HEAD_DIM = 256
D_ATTN = ATTN_HEADS * ATTN_HEAD_DIM
N_BRANCH = 3
HGRN_CHUNK = 64
EPS = 1e-6
SPLITS = (D_CONV, D_CONV, D_CONV, D_HGRN_F, D_HGRN_F, D_HGRN, D_HGRN, D_ATTN, D_ATTN, N_BRANCH * D_MODEL)
D_IN = sum(SPLITS)

kernel_name = "hybrid_conformer_hgrn2_memattn_step"


def rmsnorm(x, g):
    xf = x.astype(jnp.float32)
    y = xf * lax.rsqrt(jnp.mean(xf * xf, axis=-1, keepdims=True) + EPS)
    return (y * g.astype(jnp.float32)).astype(x.dtype)


def layernorm(x, g, b):
    xf = x.astype(jnp.float32)
    mu = jnp.mean(xf, axis=-1, keepdims=True)
    var = jnp.mean(jnp.square(xf - mu), axis=-1, keepdims=True)
    y = (xf - mu) * lax.rsqrt(var + EPS)
    return (y * g.astype(jnp.float32) + b.astype(jnp.float32)).astype(x.dtype)


def causal_dwconv(u, tail, w, b):
    up = jnp.concatenate([tail, u], axis=1)
    y = lax.conv_general_dilated(up, w[:, None, :], window_strides=(1,), padding='VALID',
                                 dimension_numbers=('NWC', 'WIO', 'NWC'), feature_group_count=D_CONV)
    return y + b, up[:, -CONV_TAIL:]


def hgrn2_scan(q, f, v, S0):
    N, L = q.shape[0], q.shape[1]
    C = math.gcd(L, HGRN_CHUNK)
    nc = L // C
    g = jnp.log(f)
    k = 1.0 - f

    def to_chunks(t):
        return t.reshape(N, nc, C, HGRN_HEADS, t.shape[-1]).transpose(1, 0, 3, 2, 4)

    mask = jnp.tril(jnp.ones((C, C), dtype=bool))

    def step(S, xs):
        qc, kc, gc, vc = xs
        b = jnp.cumsum(gc, axis=2)
        diff = b[:, :, :, None, :] - b[:, :, None, :, :]
        decay = jnp.exp(jnp.where(mask[None, None, :, :, None], diff, -jnp.inf))
        A = jnp.einsum('nhtk,nhtsk,nhsk->nhts', qc, decay, kc)
        o = jnp.einsum('nhts,nhsv->nhtv', A, vc) + jnp.einsum('nhtk,nhkv->nhtv', qc * jnp.exp(b), S)
        bC = b[:, :, -1]
        S_new = jnp.exp(bC)[..., None] * S + jnp.einsum('nhsk,nhsv->nhkv', kc * jnp.exp(bC[:, :, None, :] - b), vc)
        return S_new, o

    S_fin, o = lax.scan(step, S0.astype(jnp.float32), (to_chunks(q), to_chunks(k), to_chunks(g), to_chunks(v)))
    o = o.transpose(1, 0, 3, 2, 4).reshape(N, L, HGRN_HEADS, HGRN_DV)
    return o, S_fin


def mem_kv(mem, g_mem, w_mem_kv):
    m = rmsnorm(mem, g_mem) @ w_mem_kv
    mk, mv = jnp.split(m, 2, axis=-1)
    N = mem.shape[0]
    return (mk.reshape(N, N_MEM, ATTN_HEADS, ATTN_HEAD_DIM), mv.reshape(N, N_MEM, ATTN_HEADS, ATTN_HEAD_DIM))


def mixer_layer(h, conv_tail, S0, mk, mv, lb, w_in, conv_w, conv_b, ln_conv_g, ln_conv_b, w_conv_out,
                g_hgrn_norm, w_hgrn_out, w_attn_out, b_gate, w_out):
    N, L = h.shape[0], h.shape[1]
    proj = h @ w_in
    idx = [int(s) for s in np.cumsum(SPLITS)[:-1]]
    c_val, c_glu, c_silu, hq, hf, hi, hog, aq, a_silu, gate_logits = jnp.split(proj, idx, axis=-1)

    u = c_val * jax.nn.sigmoid(c_glu)
    dc, new_tail = causal_dwconv(u, conv_tail, conv_w, conv_b)
    a = jax.nn.silu(layernorm(dc, ln_conv_g, ln_conv_b)) * jax.nn.silu(c_silu)
    pA = a @ w_conv_out

    f = lb + (1.0 - lb) * jax.nn.sigmoid(hf.astype(jnp.float32))
    q = hq.astype(jnp.float32).reshape(N, L, HGRN_HEADS, HGRN_DK) * (HGRN_DK ** -0.5)
    f = f.reshape(N, L, HGRN_HEADS, HGRN_DK)
    v = hi.astype(jnp.float32).reshape(N, L, HGRN_HEADS, HGRN_DV)
    o, S_new = hgrn2_scan(q, f, v, S0)
    o = rmsnorm(o, g_hgrn_norm.reshape(HGRN_HEADS, HGRN_DV)).reshape(N, L, D_HGRN).astype(h.dtype)
    pB = (o * jax.nn.silu(hog)) @ w_hgrn_out

    aqh = aq.reshape(N, L, ATTN_HEADS, ATTN_HEAD_DIM).astype(jnp.float32)
    s = jnp.einsum('nlhd,nmhd->nhlm', aqh, mk.astype(jnp.float32)) * (ATTN_HEAD_DIM ** -0.5)
    pr = jax.nn.softmax(s, axis=-1)
    ao = jnp.einsum('nhlm,nmhd->nlhd', pr, mv.astype(jnp.float32)).reshape(N, L, D_ATTN).astype(h.dtype)
    pC = (ao * jax.nn.silu(a_silu)) @ w_attn_out

    gt = jax.nn.sigmoid(gate_logits.reshape(N, L, N_BRANCH, D_MODEL) + b_gate)
    merged = gt[:, :, 0] * pA + gt[:, :, 1] * pB + gt[:, :, 2] * pC
    return merged @ w_out, new_tail, S_new.astype(S0.dtype)


def setup_inputs(seed: int = 0) -> dict:
    key = jax.random.key(seed)
    ks = jax.random.split(key, 24)
    nrm = jax.random.normal
    f32 = jnp.float32
    return {
        "x_prompt": nrm(ks[0], (BATCH, SEQ, D_MODEL), f32),
        "x_sample": nrm(ks[1], (DEC_BATCH, DEC_SEQ, D_MODEL), f32),
        "state_conv": 0.5 * nrm(ks[2], (DEPTH, DEC_BATCH, CONV_TAIL, D_CONV), f32),
        "state_hgrn": 0.5 * nrm(ks[3], (DEPTH, DEC_BATCH, HGRN_HEADS, HGRN_DK, HGRN_DV), f32),
        "cache_mem_k": nrm(ks[4], (DEPTH, DEC_BATCH, N_MEM, ATTN_HEADS, ATTN_HEAD_DIM), f32),
        "cache_mem_v": nrm(ks[5], (DEPTH, DEC_BATCH, N_MEM, ATTN_HEADS, ATTN_HEAD_DIM), f32),
        "mem_prompt": nrm(ks[6], (BATCH, N_MEM, D_MODEL), f32),
        "g_pre": 1.0 + 0.02 * nrm(ks[7], (DEPTH, D_MODEL), f32),
        "w_in": nrm(ks[8], (DEPTH, D_MODEL, D_IN), f32) * D_MODEL ** -0.5,
        "conv_w": nrm(ks[9], (DEPTH, CONV_WIDTH, D_CONV), f32) * CONV_WIDTH ** -0.5,
        "conv_b": 0.01 * nrm(ks[10], (DEPTH, D_CONV), f32),
        "ln_conv_g": 1.0 + 0.02 * nrm(ks[11], (DEPTH, D_CONV), f32),
        "ln_conv_b": 0.01 * nrm(ks[12], (DEPTH, D_CONV), f32),
        "w_conv_out": nrm(ks[13], (DEPTH, D_CONV, D_MODEL), f32) * D_CONV ** -0.5,
        "lb_logits": 0.1 * nrm(ks[14], (DEPTH + 1, D_HGRN_F), f32),
        "g_hgrn_norm": 1.0 + 0.02 * nrm(ks[15], (DEPTH, D_HGRN), f32),
        "w_hgrn_out": nrm(ks[16], (DEPTH, D_HGRN, D_MODEL), f32) * D_HGRN ** -0.5,
        "g_mem": 1.0 + 0.02 * nrm(ks[17], (DEPTH, D_MODEL), f32),
        "w_mem_kv": nrm(ks[18], (DEPTH, D_MODEL, 2 * D_ATTN), f32) * D_MODEL ** -0.5,
        "w_attn_out": nrm(ks[19], (DEPTH, D_ATTN, D_MODEL), f32) * D_ATTN ** -0.5,
        "b_gate": 0.01 * nrm(ks[20], (DEPTH, N_BRANCH, D_MODEL), f32),
        "w_out": nrm(ks[21], (DEPTH, D_MODEL, D_MODEL), f32) * D_MODEL ** -0.5,
        "g_final": 1.0 + 0.02 * nrm(ks[22], (D_MODEL,), f32),
    }


def reference(x_prompt, x_sample, state_conv, state_hgrn, cache_mem_k, cache_mem_v, mem_prompt,
              g_pre, w_in, conv_w, conv_b, ln_conv_g, ln_conv_b, w_conv_out, lb_logits, g_hgrn_norm,
              w_hgrn_out, g_mem, w_mem_kv, w_attn_out, b_gate, w_out, g_final):
    lb_all = jnp.cumsum(jax.nn.softmax(lb_logits.astype(jnp.float32), axis=0), axis=0)
    hp, hs = x_prompt, x_sample
    tails_p, S_p, mk_p, mv_p, tails_s, S_s = [], [], [], [], [], []
    for l in range(DEPTH):
        layer_w = (w_in[l], conv_w[l], conv_b[l], ln_conv_g[l], ln_conv_b[l], w_conv_out[l],
                   g_hgrn_norm[l], w_hgrn_out[l], w_attn_out[l], b_gate[l], w_out[l])
        mk, mv = mem_kv(mem_prompt, g_mem[l], w_mem_kv[l])
        tail0 = jnp.zeros((hp.shape[0], CONV_TAIL, D_CONV), hp.dtype)
        S0 = jnp.zeros((hp.shape[0], HGRN_HEADS, HGRN_DK, HGRN_DV), hp.dtype)
        yp, tp, sp = mixer_layer(rmsnorm(hp, g_pre[l]), tail0, S0, mk, mv, lb_all[l], *layer_w)
        hp = hp + yp
        ys, ts, ss = mixer_layer(rmsnorm(hs, g_pre[l]), state_conv[l], state_hgrn[l],
                                 cache_mem_k[l], cache_mem_v[l], lb_all[l], *layer_w)
        hs = hs + ys
        tails_p.append(tp); S_p.append(sp); mk_p.append(mk); mv_p.append(mv)
        tails_s.append(ts); S_s.append(ss)
    y_prompt = rmsnorm(hp, g_final)
    y_sample = rmsnorm(hs, g_final)
    return (y_prompt, y_sample, jnp.stack(tails_p), jnp.stack(S_p), jnp.stack(mk_p), jnp.stack(mv_p),
            jnp.stack(tails_s), jnp.stack(S_s))
```

```python
import functools

import jax
import jax.numpy as jnp
from jax import lax
from jax.experimental import pallas as pl
from jax.experimental.pallas import tpu as pltpu

F32 = jnp.float32
BF16 = jnp.bfloat16

D_MODEL = 2048
N_MEM = 256
D_CONV = 1024
CONV_WIDTH = 31
CONV_TAIL = CONV_WIDTH - 1
HGRN_HEADS = 8
HGRN_DK = 128
HGRN_DV = 128
D_HGRN = HGRN_HEADS * HGRN_DV
ATTN_HEADS = 4
ATTN_HEAD_DIM = 256
D_ATTN = ATTN_HEADS * ATTN_HEAD_DIM
N_BRANCH = 3
EPS = 1e-6

COL = 1024
SLAB_CONV_VAL, SLAB_CONV_GLU, SLAB_CONV_SILU = 0, 1, 2
SLAB_HGRN_Q, SLAB_HGRN_F, SLAB_HGRN_I, SLAB_HGRN_OG = 3, 4, 5, 6
SLAB_ATTN_Q, SLAB_ATTN_SILU = 7, 8
SLAB_GATE = 9

SUBLANES = 8
CONV_HALO = 32
CONV_LANES = 256
VMEM_LIMIT = 56 * 2 ** 20


def _params(*sem):
    return pltpu.CompilerParams(dimension_semantics=sem, vmem_limit_bytes=VMEM_LIMIT)


def _dot(a, b):
    return jnp.dot(a, b, preferred_element_type=F32)


def _dot_nt(a, b):
    return lax.dot_general(a, b, (((1,), (1,)), ((), ())), preferred_element_type=F32)


def _dot_tn(a, b):
    return lax.dot_general(a, b, (((0,), (0,)), ((), ())), preferred_element_type=F32)


def _silu(x):
    return x * jax.nn.sigmoid(x)


def _rmsnorm_kernel(x_ref, g_ref, o_ref):
    x = x_ref[...]
    ms = jnp.mean(x * x, axis=-1, keepdims=True)
    o_ref[...] = (x * lax.rsqrt(ms + EPS) * g_ref[...]).astype(o_ref.dtype)


def _rmsnorm_bf16(x2d, g):
    t, d = x2d.shape
    tm = 512
    return pl.pallas_call(
        _rmsnorm_kernel,
        grid=(t // tm,),
        in_specs=[pl.BlockSpec((tm, d), lambda i: (i, 0)), pl.BlockSpec((1, d), lambda i: (0, 0))],
        out_specs=pl.BlockSpec((tm, d), lambda i: (i, 0)),
        out_shape=jax.ShapeDtypeStruct((t, d), BF16),
        compiler_params=_params("arbitrary"),
        name="rmsnorm_bf16",
    )(x2d, g.reshape(1, d))


def _proj_kernel(x_ref, w_ref, o_ref, wb_ref):
    @pl.when(pl.program_id(1) == 0)
    def _():
        wb_ref[...] = w_ref[...].astype(BF16)

    o_ref[...] = _dot(x_ref[...], wb_ref[...])


def _in_proj(h, w):
    t, k = h.shape
    n = w.shape[1]
    tm = min(t, 1024)
    return pl.pallas_call(
        _proj_kernel,
        grid=(n // COL, t // tm),
        in_specs=[pl.BlockSpec((tm, k), lambda j, i: (i, 0)), pl.BlockSpec((k, COL), lambda j, i: (0, j))],
        out_specs=pl.BlockSpec((None, tm, COL), lambda j, i: (j, i, 0)),
        out_shape=jax.ShapeDtypeStruct((n // COL, t, COL), F32),
        scratch_shapes=[pltpu.VMEM((k, COL), BF16)],
        compiler_params=_params("arbitrary", "arbitrary"),
        name="in_proj",
    )(h, w)


def _conv_kernel(cv_ref, cg_ref, cs_ref, tail_ref, w_ref, b_ref, lg_ref, lb_ref, a_ref, nt_ref, up_ref, dc_ref,
                 *, nb, lt, r, ntt):
    tt = pl.program_id(1)

    @pl.when(tt == 0)
    def _():
        for s in range(nb):
            up_ref[s, CONV_HALO - CONV_TAIL:CONV_HALO, :] = tail_ref[s]

    bias = b_ref[...]
    ln_g = lg_ref[...]
    ln_b = lb_ref[...]
    win_rows = r + CONV_HALO
    for s in range(nb):
        rows = slice(s * lt, (s + 1) * lt)
        up_ref[s, CONV_HALO:CONV_HALO + lt, :] = cv_ref[rows, :] * jax.nn.sigmoid(cg_ref[rows, :])

        def sub_tile(i, carry, s=s):
            base = i * r if isinstance(i, int) else pl.multiple_of(i * r, r)
            for lc in range(D_CONV // CONV_LANES):
                lanes = slice(lc * CONV_LANES, (lc + 1) * CONV_LANES)
                win = up_ref[s, pl.ds(base, win_rows), lanes]
                acc = None
                for c in range(SUBLANES):
                    z = pltpu.roll(win, win_rows - (c + CONV_HALO - CONV_TAIL), axis=0)
                    for a in range((CONV_WIDTH - c + SUBLANES - 1) // SUBLANES):
                        j = SUBLANES * a + c
                        term = z[SUBLANES * a:SUBLANES * a + r, :] * w_ref[j:j + 1, lanes]
                        acc = term if acc is None else acc + term
                dc_ref[:, lanes] = acc
            d = dc_ref[...] + bias
            mu = jnp.mean(d, axis=-1, keepdims=True)
            xc = d - mu
            var = jnp.mean(xc * xc, axis=-1, keepdims=True)
            y = xc * lax.rsqrt(var + EPS) * ln_g + ln_b
            gate = cs_ref[pl.ds(s * lt + base, r), :]
            a_ref[pl.ds(s * lt + base, r), :] = (_silu(y) * _silu(gate)).astype(a_ref.dtype)
            return carry

        if lt == r:
            sub_tile(0, 0)
        else:
            lax.fori_loop(0, lt // r, sub_tile, 0)

    @pl.when(tt == ntt - 1)
    def _():
        for s in range(nb):
            nt_ref[s] = up_ref[s, lt + CONV_HALO - CONV_TAIL:lt + CONV_HALO, :]

    if ntt > 1:
        for s in range(nb):
            up_ref[s, 0:CONV_HALO, :] = up_ref[s, lt:lt + CONV_HALO, :]


def _conv_branch(proj, n_seq, seq_len, tail, conv_w, conv_b, ln_g, ln_b):
    t = n_seq * seq_len
    if seq_len >= 512:
        nb, lt, r = 1, 512, 32
    else:
        nb, lt, r = 16, seq_len, seq_len
    ntt = seq_len // lt
    rows = nb * lt

    def slab(c):
        return pl.BlockSpec((None, rows, COL), lambda n, tt, c=c: (c, n * ntt + tt, 0))

    vec = pl.BlockSpec((1, D_CONV), lambda n, tt: (0, 0))
    tail_spec = pl.BlockSpec((nb, CONV_TAIL, D_CONV), lambda n, tt: (n, 0, 0))
    return pl.pallas_call(
        functools.partial(_conv_kernel, nb=nb, lt=lt, r=r, ntt=ntt),
        grid=(n_seq // nb, ntt),
        in_specs=[slab(SLAB_CONV_VAL), slab(SLAB_CONV_GLU), slab(SLAB_CONV_SILU), tail_spec,
                  pl.BlockSpec((CONV_WIDTH, D_CONV), lambda n, tt: (0, 0)), vec, vec, vec],
        out_specs=[pl.BlockSpec((rows, D_CONV), lambda n, tt: (n * ntt + tt, 0)), tail_spec],
        out_shape=[jax.ShapeDtypeStruct((t, D_CONV), BF16), jax.ShapeDtypeStruct((n_seq, CONV_TAIL, D_CONV), F32)],
        scratch_shapes=[pltpu.VMEM((nb, CONV_HALO + lt, D_CONV), F32), pltpu.VMEM((r, D_CONV), F32)],
        compiler_params=_params("arbitrary", "arbitrary"),
        name="conv_branch",
    )(proj, proj, proj, tail, conv_w, conv_b.reshape(1, -1), ln_g.reshape(1, -1), ln_b.reshape(1, -1))


def _hold(b, m, c):
    if 2 * m >= SUBLANES:
        groups = c // (2 * m)
        b3 = b.reshape(groups, 2 * m, b.shape[-1])
        return jnp.broadcast_to(b3[:, m - 1:m, :], b3.shape).reshape(b.shape)
    b3 = b.reshape(c // SUBLANES, SUBLANES, b.shape[-1])
    p = lax.broadcasted_iota(jnp.int32, b3.shape, 1)
    if m == 2:
        out = jnp.where(p < 4, b3[:, 1:2, :], b3[:, 5:6, :])
    else:
        out = jnp.where(p < 2, b3[:, 0:1, :], jnp.where(p < 4, b3[:, 2:3, :],
                        jnp.where(p < 6, b3[:, 4:5, :], b3[:, 6:7, :])))
    return out.reshape(b.shape)


def _hgrn_chunk(hq, hf, hi, state, lb, level, tidx, eye, c):
    f = lb + (1.0 - lb) * jax.nn.sigmoid(hf)
    g = jnp.log(f)
    k = 1.0 - f
    q = hq * (HGRN_DK ** -0.5)
    b = g
    sh = 1
    while sh < c:
        b = b + jnp.where(tidx >= sh, pltpu.roll(b, sh, axis=0), 0.0)
        sh *= 2
    b_last = b[c - 1:c, :]
    v16 = hi.astype(BF16)

    scores = jnp.where(level == 0, _dot_nt(q.astype(BF16), k.astype(BF16)), 0.0)
    m = 1
    while m < c:
        w = jnp.exp(-jnp.abs(b - _hold(b, m, c)))
        scores = jnp.where(level == m, _dot_nt((q * w).astype(BF16), (k * w).astype(BF16)), scores)
        m *= 2

    o = _dot(scores.astype(BF16), v16) + _dot((q * jnp.exp(b)).astype(BF16), state.astype(BF16))
    k_dec = (k * jnp.exp(b_last - b)).astype(BF16)
    decay_row = jnp.exp(b_last)
    decay_col = jnp.sum(jnp.where(eye, jnp.broadcast_to(decay_row, eye.shape), 0.0), axis=1, keepdims=True)
    return o, state * decay_col + _dot_tn(k_dec, v16)


def _hgrn_kernel(q_ref, f_ref, i_ref, og_ref, s0_ref, lbl_ref, gn_ref, o_ref, sn_ref, *scratch, nb, c, nchunks):
    logits = lbl_ref[...]
    e = jnp.exp(logits - jnp.max(logits, axis=0, keepdims=True))
    lb_all = e[0:1, :] / jnp.sum(e, axis=0, keepdims=True)

    tidx = lax.broadcasted_iota(jnp.int32, (c, HGRN_DK), 0)
    ti = lax.broadcasted_iota(jnp.int32, (c, c), 0)
    si = lax.broadcasted_iota(jnp.int32, (c, c), 1)
    diff = ti ^ si
    level = jnp.where(si > ti, -1, 0)
    m = 1
    while m < c:
        level = jnp.where((si < ti) & ((diff & ~(m - 1)) == m), m, level)
        m *= 2
    eye = (lax.broadcasted_iota(jnp.int32, (HGRN_DK, HGRN_DK), 0)
           == lax.broadcasted_iota(jnp.int32, (HGRN_DK, HGRN_DK), 1))

    if nchunks > 1:
        st_ref, = scratch

        @pl.when(pl.program_id(1) == 0)
        def _():
            st_ref[...] = s0_ref[...]
        src = st_ref
    else:
        src = s0_ref

    for s in range(nb):
        rows = slice(s * c, (s + 1) * c)
        for h in range(HGRN_HEADS):
            cols = slice(h * HGRN_DK, (h + 1) * HGRN_DK)
            o, new_state = _hgrn_chunk(q_ref[rows, cols], f_ref[rows, cols], i_ref[rows, cols], src[s, h],
                                       lb_all[:, cols], level, tidx, eye, c)
            ms = jnp.mean(o * o, axis=-1, keepdims=True)
            y = o * lax.rsqrt(ms + EPS) * gn_ref[:, cols]
            o_ref[rows, cols] = (y * _silu(og_ref[rows, cols])).astype(o_ref.dtype)
            if nchunks > 1:
                st_ref[s, h] = new_state
            else:
                sn_ref[s, h] = new_state

    if nchunks > 1:
        @pl.when(pl.program_id(1) == nchunks - 1)
        def _():
            sn_ref[...] = st_ref[...]


def _hgrn_branch(proj, n_seq, seq_len, state0, lb_logits, g_norm):
    t = n_seq * seq_len
    if seq_len >= 128:
        nb, c = 1, 128
    else:
        nb, c = 2, seq_len
    nchunks = seq_len // c
    rows = nb * c

    def slab(col):
        return pl.BlockSpec((None, rows, COL), lambda n, cc, col=col: (col, n * nchunks + cc, 0))

    state_spec = pl.BlockSpec((nb, HGRN_HEADS, HGRN_DK, HGRN_DV), lambda n, cc: (n, 0, 0, 0))
    scratch = [pltpu.VMEM((nb, HGRN_HEADS, HGRN_DK, HGRN_DV), F32)] if nchunks > 1 else []
    return pl.pallas_call(
        functools.partial(_hgrn_kernel, nb=nb, c=c, nchunks=nchunks),
        grid=(n_seq // nb, nchunks),
        in_specs=[slab(SLAB_HGRN_Q), slab(SLAB_HGRN_F), slab(SLAB_HGRN_I), slab(SLAB_HGRN_OG), state_spec,
                  pl.BlockSpec(lb_logits.shape, lambda n, cc: (0, 0)),
                  pl.BlockSpec((1, D_HGRN), lambda n, cc: (0, 0))],
        out_specs=[pl.BlockSpec((rows, D_HGRN), lambda n, cc: (n * nchunks + cc, 0)), state_spec],
        out_shape=[jax.ShapeDtypeStruct((t, D_HGRN), BF16),
                   jax.ShapeDtypeStruct((n_seq, HGRN_HEADS, HGRN_DK, HGRN_DV), F32)],
        scratch_shapes=scratch,
        compiler_params=_params("arbitrary", "arbitrary"),
        name="hgrn_branch",
    )(proj, proj, proj, proj, state0, lb_logits, g_norm.reshape(1, -1))


def _attn_kernel(q_ref, as_ref, k_ref, v_ref, o_ref, kb_ref, vb_ref):
    @pl.when(pl.program_id(1) == 0)
    def _():
        kb_ref[...] = k_ref[...].astype(BF16)
        vb_ref[...] = v_ref[...].astype(BF16)

    for h in range(ATTN_HEADS):
        cols = slice(h * ATTN_HEAD_DIM, (h + 1) * ATTN_HEAD_DIM)
        s = _dot_nt(q_ref[:, cols].astype(BF16), kb_ref[:, cols]) * (ATTN_HEAD_DIM ** -0.5)
        p = jnp.exp(s - jnp.max(s, axis=-1, keepdims=True))
        pr = p / jnp.sum(p, axis=-1, keepdims=True)
        ao = _dot(pr.astype(BF16), vb_ref[:, cols])
        o_ref[:, cols] = (ao * _silu(as_ref[:, cols])).astype(o_ref.dtype)


def _attn_branch(proj, n_seq, seq_len, mem_k, mem_v, k_slab, v_slab):
    t = n_seq * seq_len
    lt = min(seq_len, 512)
    ntt = seq_len // lt

    def slab(col):
        return pl.BlockSpec((None, lt, COL), lambda n, tt, col=col: (col, n * ntt + tt, 0))

    def mem(col):
        return pl.BlockSpec((None, N_MEM, D_ATTN), lambda n, tt, col=col: (col, n, 0))

    return pl.pallas_call(
        _attn_kernel,
        grid=(n_seq, ntt),
        in_specs=[slab(SLAB_ATTN_Q), slab(SLAB_ATTN_SILU), mem(k_slab), mem(v_slab)],
        out_specs=pl.BlockSpec((lt, D_ATTN), lambda n, tt: (n * ntt + tt, 0)),
        out_shape=jax.ShapeDtypeStruct((t, D_ATTN), BF16),
        scratch_shapes=[pltpu.VMEM((N_MEM, D_ATTN), BF16), pltpu.VMEM((N_MEM, D_ATTN), BF16)],
        compiler_params=_params("arbitrary", "arbitrary"),
        name="attn_branch",
    )(proj, proj, mem_k, mem_v)


def _merge_kernel(x_ref, a_ref, b_ref, c_ref, g00, g01, g10, g11, g20, g21, bg_ref, gf_ref,
                  wa_ref, wb_ref, wc_ref, wo_ref, o_ref):
    gates = ((g00, g01), (g10, g11), (g20, g21))
    branches = ((a_ref, wa_ref), (b_ref, wb_ref), (c_ref, wc_ref))
    halves = []
    for half in range(D_MODEL // COL):
        cols = slice(half * COL, (half + 1) * COL)
        merged = None
        for br in range(N_BRANCH):
            act_ref, w_ref = branches[br]
            gate = jax.nn.sigmoid(gates[br][half][...] + bg_ref[br:br + 1, cols])
            term = gate * _dot(act_ref[...], w_ref[:, cols])
            merged = term if merged is None else merged + term
        halves.append(merged.astype(BF16))
    y = _dot(jnp.concatenate(halves, axis=-1), wo_ref[...])
    hres = x_ref[...] + y
    ms = jnp.mean(hres * hres, axis=-1, keepdims=True)
    o_ref[...] = hres * lax.rsqrt(ms + EPS) * gf_ref[...]


def _merge_out(x2d, proj, act_a, act_b, act_c, b_gate, g_final, wa, wb, wc, wo):
    t = x2d.shape[0]
    tm = 256

    def rows(width):
        return pl.BlockSpec((tm, width), lambda i: (i, 0))

    def gate(col):
        return pl.BlockSpec((None, tm, COL), lambda i, col=col: (col, i, 0))

    def resident(shape):
        return pl.BlockSpec(shape, lambda i: (0, 0), pipeline_mode=pl.Buffered(1))

    return pl.pallas_call(
        _merge_kernel,
        grid=(t // tm,),
        in_specs=[rows(D_MODEL), rows(D_CONV), rows(D_HGRN), rows(D_ATTN)]
        + [gate(SLAB_GATE + c) for c in range(2 * N_BRANCH)]
        + [resident((N_BRANCH, D_MODEL)), resident((1, D_MODEL)),
           resident((D_CONV, D_MODEL)), resident((D_HGRN, D_MODEL)), resident((D_ATTN, D_MODEL)),
           resident((D_MODEL, D_MODEL))],
        out_specs=rows(D_MODEL),
        out_shape=jax.ShapeDtypeStruct((t, D_MODEL), F32),
        compiler_params=_params("arbitrary"),
        name="merge_out",
    )(x2d, act_a, act_b, act_c, *([proj] * (2 * N_BRANCH)), b_gate, g_final.reshape(1, -1), wa, wb, wc, wo)


def _mixer_group(x, tail, state0, mem_k, mem_v, k_slab, v_slab, g_pre, w_in, conv_w, conv_b, ln_g, ln_b,
                 lb_logits, g_hgrn_norm, b_gate, g_final, wa, wb, wc, wo):
    n_seq, seq_len, d = x.shape
    x2d = x.reshape(n_seq * seq_len, d)
    proj = _in_proj(_rmsnorm_bf16(x2d, g_pre), w_in)
    act_a, new_tail = _conv_branch(proj, n_seq, seq_len, tail, conv_w, conv_b, ln_g, ln_b)
    act_b, new_state = _hgrn_branch(proj, n_seq, seq_len, state0, lb_logits, g_hgrn_norm)
    act_c = _attn_branch(proj, n_seq, seq_len, mem_k, mem_v, k_slab, v_slab)
    y = _merge_out(x2d, proj, act_a, act_b, act_c, b_gate, g_final, wa, wb, wc, wo)
    return y.reshape(x.shape), new_tail, new_state


def kernel(x_prompt, x_sample, state_conv, state_hgrn, cache_mem_k, cache_mem_v, mem_prompt, g_pre, w_in, conv_w, conv_b, ln_conv_g, ln_conv_b, w_conv_out, lb_logits, g_hgrn_norm, w_hgrn_out, g_mem, w_mem_kv, w_attn_out, b_gate, w_out, g_final):
    depth = w_in.shape[0]
    assert depth == 1, "the final norm is fused into the (single) layer"
    n_p = x_prompt.shape[0]
    n_s = x_sample.shape[0]
    layer = 0
    wa = w_conv_out[layer].astype(BF16)
    wb = w_hgrn_out[layer].astype(BF16)
    wc = w_attn_out[layer].astype(BF16)
    wo = w_out[layer].astype(BF16)
    shared = (g_pre[layer], w_in[layer], conv_w[layer], conv_b[layer], ln_conv_g[layer], ln_conv_b[layer],
              lb_logits, g_hgrn_norm[layer], b_gate[layer], g_final, wa, wb, wc, wo)

    mem2d = mem_prompt.reshape(n_p * N_MEM, D_MODEL)
    mem_kv = _in_proj(_rmsnorm_bf16(mem2d, g_mem[layer]), w_mem_kv[layer])
    tail0 = jnp.zeros((n_p, CONV_TAIL, D_CONV), F32)
    state0 = jnp.zeros((n_p, HGRN_HEADS, HGRN_DK, HGRN_DV), F32)
    y_p, tail_p, state_p = _mixer_group(x_prompt, tail0, state0, mem_kv, mem_kv, 0, 1, *shared)

    mem_k = cache_mem_k[layer].reshape(1, n_s * N_MEM, D_ATTN)
    mem_v = cache_mem_v[layer].reshape(1, n_s * N_MEM, D_ATTN)
    y_s, tail_s, state_s = _mixer_group(x_sample, state_conv[layer], state_hgrn[layer], mem_k, mem_v, 0, 0, *shared)

    mk = mem_kv[0].reshape(1, n_p, N_MEM, ATTN_HEADS, ATTN_HEAD_DIM)
    mv = mem_kv[1].reshape(1, n_p, N_MEM, ATTN_HEADS, ATTN_HEAD_DIM)
    return (y_p, y_s, tail_p[None], state_p[None], mk, mv, tail_s[None], state_s[None])
```
